```python
import math
import jax, jax.numpy as jnp
from jax import lax
import numpy as np

D_MODEL = 1024
BATCH = 16
SEQ = 2048
DEPTH = 4

CHUNK = 64
D_MIX = D_MODEL
GDN_HEADS = 4
GDN_HEAD_DIM = 128
GDN_WIDTH = GDN_HEADS * GDN_HEAD_DIM
GDN_CONV = 4
CONV_WIDTH = D_MIX - GDN_WIDTH
CONV_KERNEL = 31
D_FF = 2816
NORM_EPS = 1e-6

Q0 = 0
K0 = Q0 + GDN_WIDTH
V0 = K0 + GDN_WIDTH
Z0 = V0 + GDN_WIDTH
A0 = Z0 + GDN_WIDTH
B0 = A0 + GDN_HEADS
G0 = B0 + GDN_HEADS
IN_COLS = G0 + 2 * CONV_WIDTH

kernel_name = "hybrid_gdn_conformer_conv_macaron"


def rms_norm(x, g):
    xf = x.astype(jnp.float32)
    y = xf * lax.rsqrt(jnp.mean(xf * xf, axis=-1, keepdims=True) + NORM_EPS)
    return (y * g.astype(jnp.float32)).astype(x.dtype)


def layer_norm(x, g, b):
    xf = x.astype(jnp.float32)
    mu = jnp.mean(xf, axis=-1, keepdims=True)
    xc = xf - mu
    y = xc * lax.rsqrt(jnp.mean(xc * xc, axis=-1, keepdims=True) + NORM_EPS)
    return (y * g.astype(jnp.float32) + b.astype(jnp.float32)).astype(x.dtype)


def causal_depthwise_conv(x, w):
    K, C = w.shape
    xp = jnp.pad(x, ((0, 0), (K - 1, 0), (0, 0)))
    return lax.conv_general_dilated(xp, w[:, None, :].astype(x.dtype), window_strides=(1,), padding='VALID',
                                    dimension_numbers=('NWC', 'WIO', 'NWC'), feature_group_count=C)


def swiglu_ffn(x, w_in, w_out):
    h = x @ w_in
    gate, up = h[..., :D_FF], h[..., D_FF:]
    return (jax.nn.silu(gate) * up) @ w_out


def l2_normalize(x):
    return x * lax.rsqrt(jnp.sum(x * x, axis=-1, keepdims=True) + 1e-6)


def chunk_gated_delta_rule(q, k, v, g, beta):
    Bsz, T, H, Dk = q.shape
    Dv = v.shape[-1]
    N = T // CHUNK

    def blocks(t):
        t = t.reshape((Bsz, N, CHUNK, H) + t.shape[3:])
        return jnp.moveaxis(t, 3, 1)

    q, k, v, g, beta = blocks(q), blocks(k), blocks(v), blocks(g), blocks(beta)
    gc = jnp.cumsum(g, axis=-1)
    idx = jnp.arange(CHUNK)
    lower = idx[:, None] >= idx[None, :]
    strict = idx[:, None] > idx[None, :]
    decay = jnp.exp(jnp.where(lower, gc[..., :, None] - gc[..., None, :], -jnp.inf))
    kb = k * beta[..., None]
    a_mat = jnp.where(strict, jnp.einsum('bhncd,bhnsd->bhncs', kb, k) * decay, 0.0)
    rhs = jnp.concatenate([v * beta[..., None], kb * jnp.exp(gc)[..., None]], axis=-1)
    sol = lax.linalg.triangular_solve(a_mat, rhs, left_side=True, lower=True, unit_diagonal=True)
    u, w = sol[..., :Dv], sol[..., Dv:]
    qk = jnp.einsum('bhncd,bhnsd->bhncs', q, k) * decay
    g_last = gc[..., -1]
    q_dec = q * jnp.exp(gc)[..., None]
    k_dec = k * jnp.exp(g_last[..., None] - gc)[..., None]
    xs = tuple(jnp.moveaxis(t, 2, 0) for t in (q_dec, k_dec, u, w, qk, jnp.exp(g_last)))

    def step(S, inp):
        qd, kd, uc, wc, qkc, eg = inp
        v_new = uc - jnp.einsum('bhcd,bhde->bhce', wc, S)
        o = jnp.einsum('bhcd,bhde->bhce', qd, S) + jnp.einsum('bhcs,bhse->bhce', qkc, v_new)
        S = S * eg[..., None, None] + jnp.einsum('bhcd,bhce->bhde', kd, v_new)
        return S, o

    S0 = jnp.zeros((Bsz, H, Dk, Dv), q.dtype)
    _, o = lax.scan(step, S0, xs)
    o = jnp.transpose(o, (1, 0, 3, 2, 4))
    return o.reshape(Bsz, T, H, Dv)


def gated_deltanet_group(h, conv_w, a_log, dt_bias, out_g):
    Bsz, T, _ = h.shape
    qkv = jax.nn.silu(causal_depthwise_conv(h[..., Q0:Z0], conv_w))
    qkv = qkv.astype(jnp.float32).reshape(Bsz, T, 3, GDN_HEADS, GDN_HEAD_DIM)
    q = l2_normalize(qkv[:, :, 0]) * (GDN_HEAD_DIM ** -0.5)
    k = l2_normalize(qkv[:, :, 1])
    v = qkv[:, :, 2]
    a = h[..., A0:B0].astype(jnp.float32)
    b = h[..., B0:G0].astype(jnp.float32)
    g = -jnp.exp(a_log.astype(jnp.float32)) * jax.nn.softplus(a + dt_bias.astype(jnp.float32))
    beta = jax.nn.sigmoid(b)
    o = chunk_gated_delta_rule(q, k, v, g, beta)
    z = h[..., Z0:A0].astype(jnp.float32).reshape(Bsz, T, GDN_HEADS, GDN_HEAD_DIM)
    o = o * lax.rsqrt(jnp.mean(o * o, axis=-1, keepdims=True) + NORM_EPS) * out_g.astype(jnp.float32)
    o = o * jax.nn.silu(z)
    return o.reshape(Bsz, T, GDN_WIDTH).astype(h.dtype)


def conformer_conv_group(h, dw_w, dw_b, ln_g, ln_b):
    glu = h[..., G0:G0 + CONV_WIDTH] * jax.nn.sigmoid(h[..., G0 + CONV_WIDTH:IN_COLS])
    u = causal_depthwise_conv(glu, dw_w) + dw_b
    return jax.nn.silu(layer_norm(u, ln_g, ln_b))


def setup_inputs(seed: int = 0) -> dict:
    key = jax.random.key(seed)
    ks = jax.random.split(key, 24)
    L, D = DEPTH, D_MODEL
    nrm = lambda k, s, fan: jax.random.normal(k, s, jnp.float32) * (fan ** -0.5)
    gain = lambda k, s: 1.0 + 0.05 * jax.random.normal(k, s, jnp.float32)
    dt = jnp.exp(jax.random.uniform(ks[9], (L, GDN_HEADS), jnp.float32, math.log(1e-3), math.log(1e-1)))
    return {
        "x": jax.random.normal(ks[0], (BATCH, SEQ, D), jnp.float32),
        "ffn1_norm_pre": gain(ks[1], (L, D)),
        "ffn1_w_in": nrm(ks[2], (L, D, 2 * D_FF), D),
        "ffn1_w_out": nrm(ks[3], (L, D_FF, D), D_FF),
        "ffn1_norm_post": gain(ks[4], (L, D)),
        "mix_norm_pre": gain(ks[5], (L, D)),
        "mix_w_in": nrm(ks[6], (L, D, IN_COLS), D),
        "gdn_conv_w": nrm(ks[7], (L, GDN_CONV, 3 * GDN_WIDTH), GDN_CONV),
        "gdn_a_log": jnp.log(jax.random.uniform(ks[8], (L, GDN_HEADS), jnp.float32, 1.0, 16.0)),
        "gdn_dt_bias": dt + jnp.log(-jnp.expm1(-dt)),
        "gdn_out_norm": gain(ks[10], (L, GDN_HEAD_DIM)),
        "cm_dw_w": nrm(ks[11], (L, CONV_KERNEL, CONV_WIDTH), CONV_KERNEL),
        "cm_dw_b": 0.02 * jax.random.normal(ks[12], (L, CONV_WIDTH), jnp.float32),
        "cm_ln_g": gain(ks[13], (L, CONV_WIDTH)),
        "cm_ln_b": 0.02 * jax.random.normal(ks[14], (L, CONV_WIDTH), jnp.float32),
        "mix_w_out": nrm(ks[15], (L, D_MIX, D), D_MIX),
        "mix_norm_post": gain(ks[16], (L, D)),
        "ffn2_norm_pre": gain(ks[17], (L, D)),
        "ffn2_w_in": nrm(ks[18], (L, D, 2 * D_FF), D),
        "ffn2_w_out": nrm(ks[19], (L, D_FF, D), D_FF),
        "ffn2_norm_post": gain(ks[20], (L, D)),
        "block_norm": gain(ks[21], (L, D)),
    }


def reference(x, ffn1_norm_pre, ffn1_w_in, ffn1_w_out, ffn1_norm_post, mix_norm_pre, mix_w_in,
              gdn_conv_w, gdn_a_log, gdn_dt_bias, gdn_out_norm, cm_dw_w, cm_dw_b, cm_ln_g, cm_ln_b,
              mix_w_out, mix_norm_post, ffn2_norm_pre, ffn2_w_in, ffn2_w_out, ffn2_norm_post, block_norm):
    for l in range(DEPTH):
        h = swiglu_ffn(rms_norm(x, ffn1_norm_pre[l]), ffn1_w_in[l], ffn1_w_out[l])
        x = x + 0.5 * rms_norm(h, ffn1_norm_post[l])
        h = rms_norm(x, mix_norm_pre[l]) @ mix_w_in[l]
        o_gdn = gated_deltanet_group(h, gdn_conv_w[l], gdn_a_log[l], gdn_dt_bias[l], gdn_out_norm[l])
        o_cm = conformer_conv_group(h, cm_dw_w[l], cm_dw_b[l], cm_ln_g[l], cm_ln_b[l])
        h = jnp.concatenate([o_gdn, o_cm], axis=-1) @ mix_w_out[l]
        x = x + rms_norm(h, mix_norm_post[l])
        h = swiglu_ffn(rms_norm(x, ffn2_norm_pre[l]), ffn2_w_in[l], ffn2_w_out[l])
        x = x + 0.5 * rms_norm(h, ffn2_norm_post[l])
        x = rms_norm(x, block_norm[l])
    return x
```

```python
import functools

import jax
import jax.numpy as jnp
from jax import lax
from jax.experimental import pallas as pl
from jax.experimental.pallas import tpu as pltpu

F32 = jnp.float32
BF16 = jnp.bfloat16

NORM_EPS = 1e-6
L2_EPS = 1e-6

GDN_HEADS = 4
HEAD_DIM = 128
GDN_WIDTH = GDN_HEADS * HEAD_DIM
GDN_CONV = 4
CONV_WIDTH = 512
CONV_KERNEL = 31
LANES = 128
SUBLANES = 8

GDN_BLOCK = 128
QKV_HALO = SUBLANES
GLU_HALO = 32

C_QKV = 0
C_Z = 3 * GDN_WIDTH
C_GLU = C_Z + GDN_WIDTH
C_AB = C_GLU + 2 * CONV_WIDTH
MIX_COLS = C_AB + LANES

VMEM_LIMIT_BYTES = 56 * 1024 * 1024


def _rms(x, g):
    ms = jnp.mean(x * x, axis=-1, keepdims=True)
    return x * lax.rsqrt(ms + NORM_EPS) * g


def _silu(x):
    return x * jax.nn.sigmoid(x)


def _dot(a, b):
    return jnp.dot(a, b, preferred_element_type=F32)


def _dot_nt(a, b):
    return lax.dot_general(a, b, (((1,), (1,)), ((), ())), preferred_element_type=F32)


def _dot_tn(a, b):
    return lax.dot_general(a, b, (((0,), (0,)), ((), ())), preferred_element_type=F32)


def _dot_f32(a, b):
    return jnp.dot(a, b, preferred_element_type=F32, precision=lax.Precision.HIGHEST)


def _ffn_body(x_ref, gpre_ref, win_ref, wout_ref, gpost_ref, gfin_ref, o_ref, *, d_ff, final_norm):
    x = x_ref[...]
    xn = _rms(x, gpre_ref[...]).astype(BF16)
    gate = _dot(xn, win_ref[:, :d_ff])
    up = _dot(xn, win_ref[:, d_ff:])
    act = (_silu(gate) * up).astype(BF16)
    h = _dot(act, wout_ref[...])
    y = x + 0.5 * _rms(h, gpost_ref[...])
    if final_norm:
        y = _rms(y, gfin_ref[...])
    o_ref[...] = y


def _ffn_call(x2d, gpre, w_in, w_out, gpost, gfin, layer, *, tm, final_norm):
    m, d = x2d.shape
    d_ff = w_out.shape[1]
    vec = pl.BlockSpec((None, 1, d), lambda i: (layer, 0, 0))
    return pl.pallas_call(
        functools.partial(_ffn_body, d_ff=d_ff, final_norm=final_norm),
        grid=(m // tm,),
        in_specs=[
            pl.BlockSpec((tm, d), lambda i: (i, 0)),
            vec,
            pl.BlockSpec((None, d, 2 * d_ff), lambda i: (layer, 0, 0), pipeline_mode=pl.Buffered(1)),
            pl.BlockSpec((None, d_ff, d), lambda i: (layer, 0, 0), pipeline_mode=pl.Buffered(1)),
            vec,
            vec,
        ],
        out_specs=pl.BlockSpec((tm, d), lambda i: (i, 0)),
        out_shape=jax.ShapeDtypeStruct((m, d), F32),
        compiler_params=pltpu.CompilerParams(
            dimension_semantics=("parallel",), vmem_limit_bytes=VMEM_LIMIT_BYTES),
        name="ffn",
    )(x2d, gpre, w_in, w_out, gpost, gfin)


def _unit_lower_inverse(a, eye):
    n = a.shape[0]
    p = -a
    t = eye + p
    covered = 2
    while covered < n:
        p = _dot_f32(p, p)
        t = t + _dot_f32(t, p)
        covered *= 2
    return t


def _mixer_body(x_ref, gpre_ref, win_ref, convw_ref, alog_ref, dtb_ref, outg_ref, dww_ref, dwb_ref,
                lng_ref, lnb_ref, wout_ref, gpost_ref, o_ref,
                qkv_ext, glu_ext, s_ref, og_ref, *, tt):
    t_idx = pl.program_id(1)

    @pl.when(t_idx == 0)
    def _():
        qkv_ext[0:QKV_HALO, :] = jnp.zeros((QKV_HALO, 3 * GDN_WIDTH), F32)
        glu_ext[0:GLU_HALO, :] = jnp.zeros((GLU_HALO, CONV_WIDTH), F32)
        s_ref[...] = jnp.zeros_like(s_ref)

    x = x_ref[...]
    xn = _rms(x, gpre_ref[...]).astype(BF16)

    qkv_ext[QKV_HALO:QKV_HALO + tt, :] = _dot(xn, win_ref[:, C_QKV:C_Z])
    acc = None
    for j in range(GDN_CONV):
        start = QKV_HALO - (GDN_CONV - 1) + j
        term = convw_ref[j:j + 1, :] * qkv_ext[start:start + tt, :]
        acc = term if acc is None else acc + term
    qkv_ext[0:QKV_HALO, :] = qkv_ext[tt:tt + QKV_HALO, :]
    qkv = _silu(acc)

    hab = _dot(xn, win_ref[:, C_AB:MIX_COLS])
    pre = hab + dtb_ref[...]
    softplus = jnp.maximum(pre, 0.0) + jnp.log1p(jnp.exp(-jnp.abs(pre)))
    g_all = -jnp.exp(alog_ref[...]) * softplus
    beta_all = jax.nn.sigmoid(hab)

    ri = lax.broadcasted_iota(jnp.int32, (tt, tt), 0)
    ci = lax.broadcasted_iota(jnp.int32, (tt, tt), 1)
    same = (ri // GDN_BLOCK) == (ci // GDN_BLOCK)
    low_inc = jnp.where(same & (ri >= ci), 1.0, 0.0).astype(F32)
    up_strict = jnp.where(same & (ci > ri), 1.0, 0.0).astype(F32)
    gc = _dot_f32(low_inc, g_all)
    rc = _dot_f32(up_strict, g_all)
    gc_t = gc.T
    e_gc = jnp.exp(gc)
    e_rc = jnp.exp(rc)

    bi = lax.broadcasted_iota(jnp.int32, (GDN_BLOCK, GDN_BLOCK), 0)
    bj = lax.broadcasted_iota(jnp.int32, (GDN_BLOCK, GDN_BLOCK), 1)
    lower = bi >= bj
    strict = bi > bj
    eye = jnp.where(bi == bj, 1.0, 0.0).astype(F32)

    z = _dot(xn, win_ref[:, C_Z:C_GLU])

    for c in range(tt // GDN_BLOCK):
        r0 = c * GDN_BLOCK
        rows = slice(r0, r0 + GDN_BLOCK)
        for h in range(GDN_HEADS):
            q = qkv[rows, h * HEAD_DIM:(h + 1) * HEAD_DIM]
            k = qkv[rows, GDN_WIDTH + h * HEAD_DIM:GDN_WIDTH + (h + 1) * HEAD_DIM]
            v = qkv[rows, 2 * GDN_WIDTH + h * HEAD_DIM:2 * GDN_WIDTH + (h + 1) * HEAD_DIM]
            q = q * lax.rsqrt(jnp.sum(q * q, axis=-1, keepdims=True) + L2_EPS) * (HEAD_DIM ** -0.5)
            k = k * lax.rsqrt(jnp.sum(k * k, axis=-1, keepdims=True) + L2_EPS)
            beta = beta_all[rows, GDN_HEADS + h:GDN_HEADS + h + 1]
            gcol = gc[rows, h:h + 1]
            grow = gc_t[h:h + 1, rows]
            decay = jnp.exp(jnp.where(lower, gcol - grow, -jnp.inf))
            kb = k * beta
            k16 = k.astype(BF16)
            a_mat = jnp.where(strict, _dot_nt(kb.astype(BF16), k16) * decay, 0.0)
            qk = _dot_nt(q.astype(BF16), k16) * decay
            t_inv = _unit_lower_inverse(a_mat, eye)
            u = _dot_f32(t_inv, v * beta)
            w = _dot_f32(t_inv, kb * e_gc[rows, h:h + 1])
            q_dec = q * e_gc[rows, h:h + 1]
            k_dec = k * e_rc[rows, h:h + 1]
            e_last = e_gc[r0 + GDN_BLOCK - 1:r0 + GDN_BLOCK, h:h + 1]

            s = s_ref[h]
            s16 = s.astype(BF16)
            v_new = u - _dot(w.astype(BF16), s16)
            o = _dot(q_dec.astype(BF16), s16) + _dot(qk.astype(BF16), v_new.astype(BF16))
            s_ref[h] = s * e_last + _dot_tn(k_dec.astype(BF16), v_new.astype(BF16))

            o = o * lax.rsqrt(jnp.mean(o * o, axis=-1, keepdims=True) + NORM_EPS) * outg_ref[...]
            o = o * _silu(z[rows, h * HEAD_DIM:(h + 1) * HEAD_DIM])
            og_ref[rows, h * HEAD_DIM:(h + 1) * HEAD_DIM] = o.astype(BF16)

    hg = _dot(xn, win_ref[:, C_GLU:C_AB])
    glu_ext[GLU_HALO:GLU_HALO + tt, :] = hg[:, :CONV_WIDTH] * jax.nn.sigmoid(hg[:, CONV_WIDTH:])
    acc = None
    for j in range(CONV_KERNEL):
        start = GLU_HALO - (CONV_KERNEL - 1) + j
        term = dww_ref[j:j + 1, :] * glu_ext[start:start + tt, :]
        acc = term if acc is None else acc + term
    glu_ext[0:GLU_HALO, :] = glu_ext[tt:tt + GLU_HALO, :]
    uc = acc + dwb_ref[...]
    mu = jnp.mean(uc, axis=-1, keepdims=True)
    xc = uc - mu
    ln = xc * lax.rsqrt(jnp.mean(xc * xc, axis=-1, keepdims=True) + NORM_EPS) * lng_ref[...] + lnb_ref[...]
    o_cm = _silu(ln).astype(BF16)

    h_out = _dot(og_ref[...], wout_ref[0:GDN_WIDTH, :]) + _dot(o_cm, wout_ref[GDN_WIDTH:, :])
    o_ref[...] = x + _rms(h_out, gpost_ref[...])


def _mixer_call(x, gpre, w_in, convw, alog, dtb, outg, dww, dwb, lng, lnb, w_out, gpost, layer, *, tt):
    b, t, d = x.shape

    def per_layer(shape, **kw):
        return pl.BlockSpec((None,) + shape, lambda bi, ti: (layer,) + (0,) * len(shape), **kw)

    return pl.pallas_call(
        functools.partial(_mixer_body, tt=tt),
        grid=(b, t // tt),
        in_specs=[
            pl.BlockSpec((None, tt, d), lambda bi, ti: (bi, ti, 0)),
            per_layer((1, d)),
            per_layer((d, MIX_COLS), pipeline_mode=pl.Buffered(1)),
            per_layer((GDN_CONV, 3 * GDN_WIDTH)),
            per_layer((1, LANES)),
            per_layer((1, LANES)),
            per_layer((1, HEAD_DIM)),
            per_layer((CONV_KERNEL, CONV_WIDTH)),
            per_layer((1, CONV_WIDTH)),
            per_layer((1, CONV_WIDTH)),
            per_layer((1, CONV_WIDTH)),
            per_layer((d, d), pipeline_mode=pl.Buffered(1)),
            per_layer((1, d)),
        ],
        out_specs=pl.BlockSpec((None, tt, d), lambda bi, ti: (bi, ti, 0)),
        out_shape=jax.ShapeDtypeStruct((b, t, d), F32),
        scratch_shapes=[
            pltpu.VMEM((QKV_HALO + tt, 3 * GDN_WIDTH), F32),
            pltpu.VMEM((GLU_HALO + tt, CONV_WIDTH), F32),
            pltpu.VMEM((GDN_HEADS, HEAD_DIM, HEAD_DIM), F32),
            pltpu.VMEM((tt, GDN_WIDTH), BF16),
        ],
        compiler_params=pltpu.CompilerParams(
            dimension_semantics=("parallel", "arbitrary"), vmem_limit_bytes=VMEM_LIMIT_BYTES),
        name="mixer",
    )(x, gpre, w_in, convw, alog, dtb, outg, dww, dwb, lng, lnb, w_out, gpost)


def _pack_mix_w_in(w):
    l, d, _ = w.shape
    a0 = 4 * GDN_WIDTH
    g0 = a0 + 2 * GDN_HEADS
    pad = jnp.zeros((l, d, LANES - 2 * GDN_HEADS), w.dtype)
    return jnp.concatenate([w[..., :a0], w[..., g0:], w[..., a0:g0], pad], axis=-1).astype(BF16)


def _lane_pad(v):
    l, n = v.shape
    return jnp.concatenate([v, jnp.zeros((l, LANES - n), v.dtype)], axis=-1)[:, None, :]


def _forward(x, ffn1_norm_pre, ffn1_w_in, ffn1_w_out, ffn1_norm_post, mix_norm_pre, mix_w_in,
             gdn_conv_w, gdn_a_log, gdn_dt_bias, gdn_out_norm, cm_dw_w, cm_dw_b, cm_ln_g, cm_ln_b,
             mix_w_out, mix_norm_post, ffn2_norm_pre, ffn2_w_in, ffn2_w_out, ffn2_norm_post, block_norm,
             *, tm, tt):
    b, t, d = x.shape
    depth = ffn1_w_in.shape[0]
    row = lambda v: v[:, None, :]
    f1_in, f1_out = ffn1_w_in.astype(BF16), ffn1_w_out.astype(BF16)
    f2_in, f2_out = ffn2_w_in.astype(BF16), ffn2_w_out.astype(BF16)
    m_in, m_out = _pack_mix_w_in(mix_w_in), mix_w_out.astype(BF16)
    alog, dtb = _lane_pad(gdn_a_log), _lane_pad(gdn_dt_bias)
    for l in range(depth):
        x2 = _ffn_call(x.reshape(b * t, d), row(ffn1_norm_pre), f1_in, f1_out, row(ffn1_norm_post),
                       row(block_norm), l, tm=tm, final_norm=False)
        x = _mixer_call(x2.reshape(b, t, d), row(mix_norm_pre), m_in, gdn_conv_w, alog, dtb,
                        row(gdn_out_norm), cm_dw_w, row(cm_dw_b), row(cm_ln_g), row(cm_ln_b),
                        m_out, row(mix_norm_post), l, tt=tt)
        x2 = _ffn_call(x.reshape(b * t, d), row(ffn2_norm_pre), f2_in, f2_out, row(ffn2_norm_post),
                       row(block_norm), l, tm=tm, final_norm=True)
        x = x2.reshape(b, t, d)
    return x


def kernel(x, ffn1_norm_pre, ffn1_w_in, ffn1_w_out, ffn1_norm_post, mix_norm_pre, mix_w_in, gdn_conv_w,
           gdn_a_log, gdn_dt_bias, gdn_out_norm, cm_dw_w, cm_dw_b, cm_ln_g, cm_ln_b, mix_w_out,
           mix_norm_post, ffn2_norm_pre, ffn2_w_in, ffn2_w_out, ffn2_norm_post, block_norm):
    return _forward(x, ffn1_norm_pre, ffn1_w_in, ffn1_w_out, ffn1_norm_post, mix_norm_pre, mix_w_in,
                    gdn_conv_w, gdn_a_log, gdn_dt_bias, gdn_out_norm, cm_dw_w, cm_dw_b, cm_ln_g, cm_ln_b,
                    mix_w_out, mix_norm_post, ffn2_norm_pre, ffn2_w_in, ffn2_w_out, ffn2_norm_post,
                    block_norm, tm=512, tt=256)
```

```python
import functools

import jax
import jax.numpy as jnp
from jax import lax
from jax.experimental import pallas as pl
from jax.experimental.pallas import tpu as pltpu

F32 = jnp.float32
BF16 = jnp.bfloat16

NORM_EPS = 1e-6
L2_EPS = 1e-6

GDN_HEADS = 4
HEAD_DIM = 128
GDN_WIDTH = GDN_HEADS * HEAD_DIM
GDN_CONV = 4
CONV_WIDTH = 512
CONV_KERNEL = 31
LANES = 128
SUBLANES = 8

GDN_BLOCK = 128
QKV_HALO = SUBLANES
GLU_HALO = 32
CONV_ROWS = 32
CONV_COLS = 256

C_QKV = 0
C_Z = 3 * GDN_WIDTH
C_GLU = C_Z + GDN_WIDTH
C_AB = C_GLU + 2 * CONV_WIDTH
MIX_COLS = C_AB + LANES

VMEM_LIMIT_BYTES = 56 * 1024 * 1024


def _rms(x, g):
    ms = jnp.mean(x * x, axis=-1, keepdims=True)
    return x * lax.rsqrt(ms + NORM_EPS) * g


def _silu(x):
    return x * jax.nn.sigmoid(x)


def _dot(a, b):
    return jnp.dot(a, b, preferred_element_type=F32)


def _dot_nt(a, b):
    return lax.dot_general(a, b, (((1,), (1,)), ((), ())), preferred_element_type=F32)


def _ffn_body(x_ref, gpre_ref, win_ref, wout_ref, gpost_ref, gfin_ref, o_ref, *, d_ff, final_norm):
    x = x_ref[...]
    xn = _rms(x, gpre_ref[...]).astype(BF16)
    gate = _dot(xn, win_ref[:, :d_ff])
    up = _dot(xn, win_ref[:, d_ff:])
    act = (_silu(gate) * up).astype(BF16)
    h = _dot(act, wout_ref[...])
    y = x + 0.5 * _rms(h, gpost_ref[...])
    if final_norm:
        y = _rms(y, gfin_ref[...])
    o_ref[...] = y


def _ffn_call(x2d, gpre, w_in, w_out, gpost, gfin, layer, *, tm, final_norm):
    m, d = x2d.shape
    d_ff = w_out.shape[1]
    vec = pl.BlockSpec((None, 1, d), lambda i: (layer, 0, 0))
    return pl.pallas_call(
        functools.partial(_ffn_body, d_ff=d_ff, final_norm=final_norm),
        grid=(m // tm,),
        in_specs=[
            pl.BlockSpec((tm, d), lambda i: (i, 0)),
            vec,
            pl.BlockSpec((None, d, 2 * d_ff), lambda i: (layer, 0, 0), pipeline_mode=pl.Buffered(1)),
            pl.BlockSpec((None, d_ff, d), lambda i: (layer, 0, 0), pipeline_mode=pl.Buffered(1)),
            vec,
            vec,
        ],
        out_specs=pl.BlockSpec((tm, d), lambda i: (i, 0)),
        out_shape=jax.ShapeDtypeStruct((m, d), F32),
        compiler_params=pltpu.CompilerParams(
            dimension_semantics=("parallel",), vmem_limit_bytes=VMEM_LIMIT_BYTES),
        name="ffn",
    )(x2d, gpre, w_in, w_out, gpost, gfin)


def _dot16(a, b):
    return _dot(a.astype(BF16), b.astype(BF16))


def _split3(x):
    x1 = x.astype(BF16)
    r1 = x - x1.astype(F32)
    x2 = r1.astype(BF16)
    x3 = (r1 - x2.astype(F32)).astype(BF16)
    return x1, x2, x3


def _unit_lower_inverses(a_list):
    n = a_list[0].shape[0]
    bi = lax.broadcasted_iota(jnp.int32, (n, n), 0)
    bj = lax.broadcasted_iota(jnp.int32, (n, n), 1)
    base = SUBLANES
    same = (bi // base) == (bj // base)
    p = [jnp.where(same, -a, 0.0) for a in a_list]
    r = p
    covered = 2
    while covered < base:
        p = [_dot16(x, x) for x in p]
        r = [ri + pi + _dot16(ri, pi) for ri, pi in zip(r, p)]
        covered *= 2
    b = base
    while b < n:
        off = ((bi // (2 * b)) == (bj // (2 * b))) & ((bi // b) != (bj // b))
        r16 = [ri.astype(BF16) for ri in r]
        a_off = [jnp.where(off, a, 0.0) for a in a_list]
        x = [ao + _dot(ao.astype(BF16), ri) for ao, ri in zip(a_off, r16)]
        r = [ri - xi - _dot(ri16, xi.astype(BF16)) for ri, ri16, xi in zip(r, r16, x)]
        b *= 2
    return r


def _causal_dw_conv(ext_ref, w_ref, r0, c0, *, taps, halo):
    cols = slice(c0, c0 + CONV_COLS)
    n_rows = CONV_ROWS + halo
    base = ext_ref[r0:r0 + n_rows, cols]
    first = halo - (taps - 1)
    acc = None
    for phase in range(SUBLANES):
        shifts = [s for s in range(first, first + taps) if s % SUBLANES == phase]
        if not shifts:
            continue
        src = base if phase == 0 else pltpu.roll(base, n_rows - phase, axis=0)
        for s in shifts:
            off = s - phase
            j = s - first
            w_tile = w_ref[j * SUBLANES:(j + 1) * SUBLANES, cols]
            term = jnp.concatenate([w_tile] * (CONV_ROWS // SUBLANES), axis=0) * src[off:off + CONV_ROWS, :]
            acc = term if acc is None else acc + term
    return acc


def _mixer_body(x_ref, gpre_ref, win_ref, convw_ref, alog_ref, dtb_ref, outg_ref, dww_ref, dwb_ref,
                lng_ref, lnb_ref, wout_ref, gpost_ref, o_ref,
                qkv_ext, glu_ext, s_ref, qkv_ref, og_ref, ocm_ref, *, tt):
    t_idx = pl.program_id(1)
    n_blk = tt // GDN_BLOCK

    @pl.when(t_idx == 0)
    def _():
        qkv_ext[0:QKV_HALO, :] = jnp.zeros((QKV_HALO, 3 * GDN_WIDTH), F32)
        glu_ext[0:GLU_HALO, :] = jnp.zeros((GLU_HALO, CONV_WIDTH), F32)
        s_ref[...] = jnp.zeros_like(s_ref)

    x = x_ref[...]
    xn = _rms(x, gpre_ref[...]).astype(BF16)

    qkv_ext[QKV_HALO:QKV_HALO + tt, :] = _dot(xn, win_ref[:, C_QKV:C_Z])
    for r0 in range(0, tt, CONV_ROWS):
        for c0 in range(0, 3 * GDN_WIDTH, CONV_COLS):
            acc = _causal_dw_conv(qkv_ext, convw_ref, r0, c0, taps=GDN_CONV, halo=QKV_HALO)
            qkv_ref[r0:r0 + CONV_ROWS, c0:c0 + CONV_COLS] = _silu(acc)
    qkv_ext[0:QKV_HALO, :] = qkv_ext[tt:tt + QKV_HALO, :]

    hab = _dot(xn, win_ref[:, C_AB:MIX_COLS])
    pre = hab + dtb_ref[...]
    softplus = jnp.maximum(pre, 0.0) + jnp.log1p(jnp.exp(-jnp.abs(pre)))
    g_all = -jnp.exp(alog_ref[...]) * softplus
    beta_all = jax.nn.sigmoid(hab)

    ri = lax.broadcasted_iota(jnp.int32, (2 * tt, tt), 0)
    ci = lax.broadcasted_iota(jnp.int32, (2 * tt, tt), 1)
    rr = jnp.where(ri >= tt, ri - tt, ri)
    same = (rr // GDN_BLOCK) == (ci // GDN_BLOCK)
    tri = ((ri < tt) & (rr >= ci)) | ((ri >= tt) & (ci > rr))
    lu = jnp.where(same & tri, 1.0, 0.0).astype(BF16)
    g1, g2, g3 = _split3(g_all)
    sums = _dot(lu, g1) + (_dot(lu, g2) + _dot(lu, g3))
    gc = sums[:tt]
    e_gc = jnp.exp(gc)
    e_rc = jnp.exp(sums[tt:])
    gc_t = gc.T
    e_rc_t = e_rc.T

    bi = lax.broadcasted_iota(jnp.int32, (GDN_BLOCK, GDN_BLOCK), 0)
    bj = lax.broadcasted_iota(jnp.int32, (GDN_BLOCK, GDN_BLOCK), 1)
    lower = bi >= bj
    strict = bi > bj

    pairs = [(c, h) for c in range(n_blk) for h in range(GDN_HEADS)]
    a_mats, qks, rhss, wq_q, kdts, e_lasts = [], [], [], [], [], []
    for c, h in pairs:
        r0 = c * GDN_BLOCK
        rows = slice(r0, r0 + GDN_BLOCK)
        q = qkv_ref[rows, h * HEAD_DIM:(h + 1) * HEAD_DIM]
        k = qkv_ref[rows, GDN_WIDTH + h * HEAD_DIM:GDN_WIDTH + (h + 1) * HEAD_DIM]
        v = qkv_ref[rows, 2 * GDN_WIDTH + h * HEAD_DIM:2 * GDN_WIDTH + (h + 1) * HEAD_DIM]
        q = q * (lax.rsqrt(jnp.sum(q * q, axis=-1, keepdims=True) + L2_EPS) * (HEAD_DIM ** -0.5))
        k = k * lax.rsqrt(jnp.sum(k * k, axis=-1, keepdims=True) + L2_EPS)
        beta = beta_all[rows, GDN_HEADS + h:GDN_HEADS + h + 1]
        gcol = gc[rows, h:h + 1]
        grow = gc_t[h:h + 1, rows]
        decay = jnp.exp(jnp.where(lower, gcol - grow, -jnp.inf))
        kb = k * beta
        k_t = k.T
        kq = _dot(jnp.concatenate([kb, q], axis=0).astype(BF16), k_t.astype(BF16))
        a_mats.append(jnp.where(strict, kq[:GDN_BLOCK] * decay, 0.0))
        qks.append((kq[GDN_BLOCK:] * decay).astype(BF16))
        e_col = e_gc[rows, h:h + 1]
        rhss.append(jnp.concatenate([v * beta, kb * e_col], axis=1))
        wq_q.append((q * e_col).astype(BF16))
        kdts.append((k_t * e_rc_t[h:h + 1, rows]).astype(BF16))
        e_lasts.append(e_gc[r0 + GDN_BLOCK - 1:r0 + GDN_BLOCK, h:h + 1])

    r_inv = _unit_lower_inverses(a_mats)
    sols = [rhs + _dot16(ri, rhs) for ri, rhs in zip(r_inv, rhss)]

    z = _dot(xn, win_ref[:, C_Z:C_GLU])
    state = [s_ref[h] for h in range(GDN_HEADS)]
    for c in range(n_blk):
        rows = slice(c * GDN_BLOCK, (c + 1) * GDN_BLOCK)
        for h in range(GDN_HEADS):
            i = c * GDN_HEADS + h
            u, w = sols[i][:, :HEAD_DIM], sols[i][:, HEAD_DIM:]
            s16 = state[h].astype(BF16)
            ws_qs = _dot(jnp.concatenate([w.astype(BF16), wq_q[i]], axis=0), s16)
            v_new = (u - ws_qs[:GDN_BLOCK]).astype(BF16)
            upd = _dot(jnp.concatenate([qks[i], kdts[i]], axis=0), v_new)
            o = ws_qs[GDN_BLOCK:] + upd[:GDN_BLOCK]
            state[h] = state[h] * e_lasts[i] + upd[GDN_BLOCK:]
            o = o * lax.rsqrt(jnp.mean(o * o, axis=-1, keepdims=True) + NORM_EPS) * outg_ref[...]
            o = o * _silu(z[rows, h * HEAD_DIM:(h + 1) * HEAD_DIM])
            og_ref[rows, h * HEAD_DIM:(h + 1) * HEAD_DIM] = o.astype(BF16)
    for h in range(GDN_HEADS):
        s_ref[h] = state[h]

    hg = _dot(xn, win_ref[:, C_GLU:C_AB])
    glu_ext[GLU_HALO:GLU_HALO + tt, :] = hg[:, :CONV_WIDTH] * jax.nn.sigmoid(hg[:, CONV_WIDTH:])
    for r0 in range(0, tt, CONV_ROWS):
        uc = jnp.concatenate(
            [_causal_dw_conv(glu_ext, dww_ref, r0, c0, taps=CONV_KERNEL, halo=GLU_HALO)
             for c0 in range(0, CONV_WIDTH, CONV_COLS)], axis=1) + dwb_ref[...]
        mu = jnp.mean(uc, axis=-1, keepdims=True)
        xc = uc - mu
        ln = xc * lax.rsqrt(jnp.mean(xc * xc, axis=-1, keepdims=True) + NORM_EPS) * lng_ref[...] + lnb_ref[...]
        ocm_ref[r0:r0 + CONV_ROWS, :] = _silu(ln).astype(BF16)
    glu_ext[0:GLU_HALO, :] = glu_ext[tt:tt + GLU_HALO, :]

    h_out = _dot(og_ref[...], wout_ref[0:GDN_WIDTH, :]) + _dot(ocm_ref[...], wout_ref[GDN_WIDTH:, :])
    o_ref[...] = x + _rms(h_out, gpost_ref[...])


def _mixer_call(x, gpre, w_in, convw, alog, dtb, outg, dww, dwb, lng, lnb, w_out, gpost, layer, *, tt):
    b, t, d = x.shape

    def per_layer(shape, **kw):
        return pl.BlockSpec((None,) + shape, lambda bi, ti: (layer,) + (0,) * len(shape), **kw)

    return pl.pallas_call(
        functools.partial(_mixer_body, tt=tt),
        grid=(b, t // tt),
        in_specs=[
            pl.BlockSpec((None, tt, d), lambda bi, ti: (bi, ti, 0)),
            per_layer((1, d)),
            per_layer((d, MIX_COLS), pipeline_mode=pl.Buffered(1)),
            per_layer((GDN_CONV * SUBLANES, 3 * GDN_WIDTH)),
            per_layer((1, LANES)),
            per_layer((1, LANES)),
            per_layer((1, HEAD_DIM)),
            per_layer((CONV_KERNEL * SUBLANES, CONV_WIDTH)),
            per_layer((1, CONV_WIDTH)),
            per_layer((1, CONV_WIDTH)),
            per_layer((1, CONV_WIDTH)),
            per_layer((d, d), pipeline_mode=pl.Buffered(1)),
            per_layer((1, d)),
        ],
        out_specs=pl.BlockSpec((None, tt, d), lambda bi, ti: (bi, ti, 0)),
        out_shape=jax.ShapeDtypeStruct((b, t, d), F32),
        scratch_shapes=[
            pltpu.VMEM((QKV_HALO + tt, 3 * GDN_WIDTH), F32),
            pltpu.VMEM((GLU_HALO + tt, CONV_WIDTH), F32),
            pltpu.VMEM((GDN_HEADS, HEAD_DIM, HEAD_DIM), F32),
            pltpu.VMEM((tt, 3 * GDN_WIDTH), F32),
            pltpu.VMEM((tt, GDN_WIDTH), BF16),
            pltpu.VMEM((tt, CONV_WIDTH), BF16),
        ],
        compiler_params=pltpu.CompilerParams(
            dimension_semantics=("parallel", "arbitrary"), vmem_limit_bytes=VMEM_LIMIT_BYTES),
        name="mixer",
    )(x, gpre, w_in, convw, alog, dtb, outg, dww, dwb, lng, lnb, w_out, gpost)


def _pack_mix_w_in(w):
    l, d, _ = w.shape
    a0 = 4 * GDN_WIDTH
    g0 = a0 + 2 * GDN_HEADS
    pad = jnp.zeros((l, d, LANES - 2 * GDN_HEADS), w.dtype)
    return jnp.concatenate([w[..., :a0], w[..., g0:], w[..., a0:g0], pad], axis=-1).astype(BF16)


def _sublane_repeat(w):
    l, k, c = w.shape
    return jnp.broadcast_to(w[:, :, None, :], (l, k, SUBLANES, c)).reshape(l, k * SUBLANES, c)


def _lane_pad(v):
    l, n = v.shape
    return jnp.concatenate([v, jnp.zeros((l, LANES - n), v.dtype)], axis=-1)[:, None, :]


def _forward(x, ffn1_norm_pre, ffn1_w_in, ffn1_w_out, ffn1_norm_post, mix_norm_pre, mix_w_in,
             gdn_conv_w, gdn_a_log, gdn_dt_bias, gdn_out_norm, cm_dw_w, cm_dw_b, cm_ln_g, cm_ln_b,
             mix_w_out, mix_norm_post, ffn2_norm_pre, ffn2_w_in, ffn2_w_out, ffn2_norm_post, block_norm,
             *, tm, tt):
    b, t, d = x.shape
    depth = ffn1_w_in.shape[0]
    row = lambda v: v[:, None, :]
    f1_in, f1_out = ffn1_w_in.astype(BF16), ffn1_w_out.astype(BF16)
    f2_in, f2_out = ffn2_w_in.astype(BF16), ffn2_w_out.astype(BF16)
    m_in, m_out = _pack_mix_w_in(mix_w_in), mix_w_out.astype(BF16)
    alog, dtb = _lane_pad(gdn_a_log), _lane_pad(gdn_dt_bias)
    conv_w, dw_w = _sublane_repeat(gdn_conv_w), _sublane_repeat(cm_dw_w)
    for l in range(depth):
        x2 = _ffn_call(x.reshape(b * t, d), row(ffn1_norm_pre), f1_in, f1_out, row(ffn1_norm_post),
                       row(block_norm), l, tm=tm, final_norm=False)
        x = _mixer_call(x2.reshape(b, t, d), row(mix_norm_pre), m_in, conv_w, alog, dtb,
                        row(gdn_out_norm), dw_w, row(cm_dw_b), row(cm_ln_g), row(cm_ln_b),
                        m_out, row(mix_norm_post), l, tt=tt)
        x2 = _ffn_call(x.reshape(b * t, d), row(ffn2_norm_pre), f2_in, f2_out, row(ffn2_norm_post),
                       row(block_norm), l, tm=tm, final_norm=True)
        x = x2.reshape(b, t, d)
    return x


def kernel(x, ffn1_norm_pre, ffn1_w_in, ffn1_w_out, ffn1_norm_post, mix_norm_pre, mix_w_in, gdn_conv_w,
           gdn_a_log, gdn_dt_bias, gdn_out_norm, cm_dw_w, cm_dw_b, cm_ln_g, cm_ln_b, mix_w_out,
           mix_norm_post, ffn2_norm_pre, ffn2_w_in, ffn2_w_out, ffn2_norm_post, block_norm):
    return _forward(x, ffn1_norm_pre, ffn1_w_in, ffn1_w_out, ffn1_norm_post, mix_norm_pre, mix_w_in,
                    gdn_conv_w, gdn_a_log, gdn_dt_bias, gdn_out_norm, cm_dw_w, cm_dw_b, cm_ln_g, cm_ln_b,
                    mix_w_out, mix_norm_post, ffn2_norm_pre, ffn2_w_in, ffn2_w_out, ffn2_norm_post,
                    block_norm, tm=512, tt=256)
```
